```python
import jax
import jax.numpy as jnp
from jax import lax
import numpy as np

D_MODEL = 1024
BATCH = 2
SEQ = 8192
DEPTH = 2
DEC_BATCH = 128
DEC_SEQ = 4
PAST_LEN = 8192
PAGE_SIZE = 128

D_MIX = D_MODEL
D_POOL = D_MIX // 4
POOL_WINDOWS = (2, 4, 8, 16)
N_POOL_GROUPS = len(POOL_WINDOWS)
POOL_GROUP = D_POOL // N_POOL_GROUPS
POOL_HIST = max(POOL_WINDOWS) - 1
D_SGU = D_MIX // 4
N_SGU_HEADS = 4
SGU_HEAD = D_SGU // N_SGU_HEADS
CHUNK = 128
N_HEADS = 8
QK_NOPE = 64
QK_ROPE = 32
QK_HEAD = QK_NOPE + QK_ROPE
V_HEAD = 64
D_ATT = N_HEADS * V_HEAD
Q_LORA = 384
KV_LORA = 256
ROPE_THETA = 10000.0
SM_SCALE = QK_HEAD ** -0.5
Q_BLOCK = 128
D_IN = D_POOL + 2 * D_SGU + Q_LORA + KV_LORA + QK_ROPE
IN_SPLITS = (D_POOL, D_POOL + D_SGU, D_POOL + 2 * D_SGU, D_POOL + 2 * D_SGU + Q_LORA, D_POOL + 2 * D_SGU + Q_LORA + KV_LORA)
D_FF = 2816
N_EXPERTS = 8
TOP_K = 2
D_FF_EXPERT = 3584
N_DENSE = (DEPTH + 1) // 2
N_MOE = DEPTH // 2
EPS = 1e-6
NEG_INF = -1e30

kernel_name = "hybrid_pool_sgu_mla_adaln_step"


def rmsnorm(x, g):
    xf = x.astype(jnp.float32)
    y = xf * lax.rsqrt(jnp.mean(xf * xf, axis=-1, keepdims=True) + EPS)
    return (y * g.astype(jnp.float32)).astype(x.dtype)


def rope(x, pos):
    half = QK_ROPE // 2
    freq = ROPE_THETA ** (-jnp.arange(half, dtype=jnp.float32) / half)
    ang = pos.astype(jnp.float32)[:, None] * freq[None, :]
    cos = jnp.cos(ang)[None, :, None, :].astype(x.dtype)
    sin = jnp.sin(ang)[None, :, None, :].astype(x.dtype)
    x1, x2 = x[..., :half], x[..., half:]
    return jnp.concatenate([x1 * cos - x2 * sin, x1 * sin + x2 * cos], axis=-1)


def modulate(c, w_ada, b_ada):
    m = jax.nn.silu(c) @ w_ada + b_ada
    return jnp.split(m[:, None, :], 6, axis=-1)


def pool_mix(p_ext, n_out, w_pool, s_pool):
    B, L, _ = p_ext.shape
    pf = p_ext.astype(jnp.float32)
    cs = jnp.concatenate([jnp.zeros((B, 1, D_POOL), jnp.float32), lax.cumsum(pf, axis=1)], axis=1)
    i = jnp.arange(L - n_out, L)
    hi = cs[:, i + 1]
    means = []
    for g, w in enumerate(POOL_WINDOWS):
        lo = jnp.maximum(i + 1 - w, 0)
        cnt = (i + 1 - lo).astype(jnp.float32)
        sl = slice(g * POOL_GROUP, (g + 1) * POOL_GROUP)
        means.append((hi[..., sl] - cs[:, lo][..., sl]) / cnt[None, :, None])
    d = (jnp.concatenate(means, axis=-1) - pf[:, L - n_out:]).reshape(B, n_out, N_POOL_GROUPS, POOL_GROUP)
    y = jnp.einsum('bsgc,gcd->bsgd', d, w_pool.astype(jnp.float32)).reshape(B, n_out, D_POOL)
    return (y * s_pool.astype(jnp.float32)).astype(p_ext.dtype)


def sgu(u, vn, w_s, b_s, n):
    B, S, _ = u.shape
    vc = vn.reshape(B, S // n, n, N_SGU_HEADS, SGU_HEAD)
    wm = (w_s * jnp.tril(jnp.ones((CHUNK, CHUNK), w_s.dtype)))[:, :n, :n]
    z = jnp.einsum('hij,bcjhd->bcihd', wm, vc) + b_s[:, :n].T[None, None, :, :, None]
    return u * z.reshape(B, S, D_SGU)


def mla_project(qa, kva, kpe, pos, g_q, w_qb, g_kv):
    B, S, _ = qa.shape
    q = (rmsnorm(qa, g_q) @ w_qb).reshape(B, S, N_HEADS, QK_HEAD)
    q_nope, q_pe = q[..., :QK_NOPE], rope(q[..., QK_NOPE:], pos)
    c_kv = rmsnorm(kva, g_kv)
    k_pe = rope(kpe[:, :, None, :], pos)[:, :, 0, :]
    return q_nope, q_pe, c_kv, k_pe


def mla_prompt_attn(q_nope, q_pe, c_kv, k_pe, w_uk, w_uv):
    B, S, H, _ = q_nope.shape
    k_nope = jnp.einsum('bsc,hcd->bshd', c_kv, w_uk)
    v = jnp.einsum('bsc,hcd->bshd', c_kv, w_uv)
    nb = S // Q_BLOCK
    qn_b = q_nope.reshape(B, nb, Q_BLOCK, H, QK_NOPE).swapaxes(0, 1)
    qp_b = q_pe.reshape(B, nb, Q_BLOCK, H, QK_ROPE).swapaxes(0, 1)
    kpos = jnp.arange(S)

    def one_block(args):
        blk, qn, qp = args
        s = (jnp.einsum('bqhd,bkhd->bhqk', qn, k_nope, preferred_element_type=jnp.float32)
             + jnp.einsum('bqhr,bkr->bhqk', qp, k_pe, preferred_element_type=jnp.float32)) * SM_SCALE
        qpos = blk * Q_BLOCK + jnp.arange(Q_BLOCK)
        s = jnp.where(kpos[None, :] <= qpos[:, None], s, NEG_INF)
        pr = jax.nn.softmax(s, axis=-1).astype(v.dtype)
        return jnp.einsum('bhqk,bkhd->bqhd', pr, v)

    o = lax.map(one_block, (jnp.arange(nb), qn_b, qp_b))
    return o.swapaxes(0, 1).reshape(B, S, H * V_HEAD)


def mla_sample_attn(q_nope, q_pe, c_kv, k_pe, past_c, past_pe, w_uk, w_uv):
    B, n, H, _ = q_nope.shape
    q_abs = jnp.einsum('bqhd,hcd->bqhc', q_nope, w_uk)
    s_past = (jnp.einsum('bqhc,bkc->bhqk', q_abs, past_c, preferred_element_type=jnp.float32)
              + jnp.einsum('bqhr,bkr->bhqk', q_pe, past_pe, preferred_element_type=jnp.float32)) * SM_SCALE
    s_new = (jnp.einsum('bqhc,bkc->bhqk', q_abs, c_kv, preferred_element_type=jnp.float32)
             + jnp.einsum('bqhr,bkr->bhqk', q_pe, k_pe, preferred_element_type=jnp.float32)) * SM_SCALE
    s_new = jnp.where(jnp.tril(jnp.ones((n, n), bool)), s_new, NEG_INF)
    P = past_c.shape[1]
    pr = jax.nn.softmax(jnp.concatenate([s_past, s_new], axis=-1), axis=-1).astype(c_kv.dtype)
    o_lat = (jnp.einsum('bhqk,bkc->bqhc', pr[..., :P], past_c)
             + jnp.einsum('bhqk,bkc->bqhc', pr[..., P:], c_kv))
    return jnp.einsum('bqhc,hcd->bqhd', o_lat, w_uv).reshape(B, n, H * V_HEAD)


def token_mixing(h, pos, pool_hist, past_c, past_pe, w_in, w_pool, s_pool, g_sgu, w_sgu, b_sgu,
                 g_q, w_qb, g_kv, w_uk, w_uv, w_out):
    S = h.shape[1]
    p, u, v, qa, kva, kpe = jnp.split(h @ w_in, IN_SPLITS, axis=-1)
    p_ext = p if pool_hist is None else jnp.concatenate([pool_hist.astype(p.dtype), p], axis=1)
    y_a = pool_mix(p_ext, S, w_pool, s_pool)
    new_pool = p_ext[:, -POOL_HIST:]
    vn = rmsnorm(v, g_sgu)
    y_b = sgu(u, vn, w_sgu, b_sgu, CHUNK if pool_hist is None else S)
    q_nope, q_pe, c_kv, k_pe = mla_project(qa, kva, kpe, pos, g_q, w_qb, g_kv)
    if past_c is None:
        y_c = mla_prompt_attn(q_nope, q_pe, c_kv, k_pe, w_uk, w_uv)
    else:
        y_c = mla_sample_attn(q_nope, q_pe, c_kv, k_pe, past_c.astype(c_kv.dtype), past_pe.astype(k_pe.dtype), w_uk, w_uv)
    y = jnp.concatenate([y_a, y_b, y_c], axis=-1) @ w_out
    return y, (c_kv, k_pe, new_pool, vn)


def swiglu(h, wg, wu, wd):
    return (jax.nn.silu(h @ wg) * (h @ wu)) @ wd


def moe(h, w_router, w_e_gate, w_e_up, w_e_down):
    B, S, D = h.shape
    t = h.reshape(B * S, D)
    logits = (t @ w_router).astype(jnp.float32)
    top_v, top_i = lax.top_k(logits, TOP_K)
    gates = jax.nn.softmax(top_v, axis=-1)
    comb = jnp.sum(jax.nn.one_hot(top_i, N_EXPERTS, dtype=jnp.float32) * gates[..., None], axis=1)
    out = jnp.zeros((B * S, D), jnp.float32)
    for e in range(N_EXPERTS):
        out = out + comb[:, e:e + 1] * swiglu(t, w_e_gate[e], w_e_up[e], w_e_down[e]).astype(jnp.float32)
    return out.astype(h.dtype).reshape(B, S, D)


def channel_mix(h, l, w_ff_gate, w_ff_up, w_ff_down, w_router, w_e_gate, w_e_up, w_e_down):
    i = l // 2
    if l % 2 == 0:
        return swiglu(h, w_ff_gate[i], w_ff_up[i], w_ff_down[i])
    return moe(h, w_router[i], w_e_gate[i], w_e_up[i], w_e_down[i])


def decoder_layer(x, c, pos, l, pool_hist, past_c, past_pe, g_mix, g_ffn, w_ada, b_ada, mix_params, ffn_params):
    sh1, sc1, gt1, sh2, sc2, gt2 = modulate(c, w_ada, b_ada)
    h = rmsnorm(x, g_mix) * (1 + sc1) + sh1
    y, st = token_mixing(h, pos, pool_hist, past_c, past_pe, *mix_params)
    x = x + gt1 * y
    h = rmsnorm(x, g_ffn) * (1 + sc2) + sh2
    x = x + gt2 * channel_mix(h, l, *ffn_params)
    return x, st


def setup_inputs(seed: int = 0) -> dict:
    key = jax.random.key(seed)
    ks = iter(jax.random.split(key, 48))
    f32 = jnp.float32

    def nrm(shape, scale=1.0):
        return scale * jax.random.normal(next(ks), shape, f32)

    def gain(shape):
        return 1.0 + 0.02 * jax.random.normal(next(ks), shape, f32)

    n_pages = PAST_LEN // PAGE_SIZE
    n_used = DEC_BATCH * n_pages
    n_pool = n_used + max(1, n_used // 4)
    page_table = jax.random.permutation(next(ks), n_pool)[:n_used].reshape(DEC_BATCH, n_pages).astype(jnp.int32)
    row_scale = (jnp.arange(CHUNK, dtype=f32) + 1.0) ** -0.5
    return {
        "x_prompt": nrm((BATCH, SEQ, D_MODEL)),
        "x_sample": nrm((DEC_BATCH, DEC_SEQ, D_MODEL)),
        "cache_kv_latent": nrm((DEPTH, n_pool, PAGE_SIZE, KV_LORA)),
        "cache_k_rope": nrm((DEPTH, n_pool, PAGE_SIZE, QK_ROPE)),
        "state_pool": nrm((DEPTH, DEC_BATCH, POOL_HIST, D_POOL)),
        "page_table": page_table,
        "c_prompt": nrm((BATCH, D_MODEL)),
        "c_sample": nrm((DEC_BATCH, D_MODEL)),
        "g_mix": gain((DEPTH, D_MODEL)),
        "g_ffn": gain((DEPTH, D_MODEL)),
        "w_ada": nrm((DEPTH, D_MODEL, 6 * D_MODEL), 0.5 * D_MODEL ** -0.5),
        "b_ada": nrm((DEPTH, 6 * D_MODEL), 0.02),
        "w_in": nrm((DEPTH, D_MODEL, D_IN), D_MODEL ** -0.5),
        "w_pool": nrm((DEPTH, N_POOL_GROUPS, POOL_GROUP, POOL_GROUP), POOL_GROUP ** -0.5),
        "s_pool": gain((DEPTH, D_POOL)),
        "g_sgu": gain((DEPTH, D_SGU)),
        "w_sgu": nrm((DEPTH, N_SGU_HEADS, CHUNK, CHUNK)) * row_scale[:, None],
        "b_sgu": nrm((DEPTH, N_SGU_HEADS, CHUNK), 0.02),
        "g_q": gain((DEPTH, Q_LORA)),
        "w_qb": nrm((DEPTH, Q_LORA, N_HEADS * QK_HEAD), Q_LORA ** -0.5),
        "g_kv": gain((DEPTH, KV_LORA)),
        "w_uk": nrm((DEPTH, N_HEADS, KV_LORA, QK_NOPE), KV_LORA ** -0.5),
        "w_uv": nrm((DEPTH, N_HEADS, KV_LORA, V_HEAD), KV_LORA ** -0.5),
        "w_out": nrm((DEPTH, D_MIX, D_MODEL), D_MIX ** -0.5),
        "w_ff_gate": nrm((N_DENSE, D_MODEL, D_FF), D_MODEL ** -0.5),
        "w_ff_up": nrm((N_DENSE, D_MODEL, D_FF), D_MODEL ** -0.5),
        "w_ff_down": nrm((N_DENSE, D_FF, D_MODEL), D_FF ** -0.5),
        "w_router": nrm((N_MOE, D_MODEL, N_EXPERTS), D_MODEL ** -0.5),
        "w_e_gate": nrm((N_MOE, N_EXPERTS, D_MODEL, D_FF_EXPERT), D_MODEL ** -0.5),
        "w_e_up": nrm((N_MOE, N_EXPERTS, D_MODEL, D_FF_EXPERT), D_MODEL ** -0.5),
        "w_e_down": nrm((N_MOE, N_EXPERTS, D_FF_EXPERT, D_MODEL), D_FF_EXPERT ** -0.5),
        "g_final": gain((D_MODEL,)),
    }


def reference(x_prompt, x_sample, cache_kv_latent, cache_k_rope, state_pool, page_table, c_prompt, c_sample,
              g_mix, g_ffn, w_ada, b_ada, w_in, w_pool, s_pool, g_sgu, w_sgu, b_sgu, g_q, w_qb, g_kv,
              w_uk, w_uv, w_out, w_ff_gate, w_ff_up, w_ff_down, w_router, w_e_gate, w_e_up, w_e_down, g_final):
    n_p = x_prompt.shape[1]
    db, n_s = x_sample.shape[0], x_sample.shape[1]
    pos_p = jnp.arange(n_p)
    pos_s = page_table.shape[1] * PAGE_SIZE + jnp.arange(n_s)
    ffn_params = (w_ff_gate, w_ff_up, w_ff_down, w_router, w_e_gate, w_e_up, w_e_down)
    xp, xs = x_prompt, x_sample
    ckv_p, kpe_p, pool_p = [], [], []
    ckv_s, kpe_s, pool_s, v_s = [], [], [], []
    for l in range(DEPTH):
        mix_l = (w_in[l], w_pool[l], s_pool[l], g_sgu[l], w_sgu[l], b_sgu[l], g_q[l], w_qb[l], g_kv[l],
                 w_uk[l], w_uv[l], w_out[l])
        xp, (a, b, c, _) = decoder_layer(xp, c_prompt, pos_p, l, None, None, None,
                                         g_mix[l], g_ffn[l], w_ada[l], b_ada[l], mix_l, ffn_params)
        ckv_p.append(a); kpe_p.append(b); pool_p.append(c)
        past_c = cache_kv_latent[l, page_table].reshape(db, -1, KV_LORA)
        past_pe = cache_k_rope[l, page_table].reshape(db, -1, QK_ROPE)
        xs, (a, b, c, d) = decoder_layer(xs, c_sample, pos_s, l, state_pool[l], past_c, past_pe,
                                         g_mix[l], g_ffn[l], w_ada[l], b_ada[l], mix_l, ffn_params)
        ckv_s.append(a); kpe_s.append(b); pool_s.append(c); v_s.append(d)
    y_prompt = rmsnorm(xp, g_final)
    y_sample = rmsnorm(xs, g_final)
    new_kv_latent_prompt = jnp.stack(ckv_p, 0)
    new_k_rope_prompt = jnp.stack(kpe_p, 0)
    new_pool_prompt = jnp.stack(pool_p, 0)
    new_kv_latent_sample = jnp.stack(ckv_s, 0)
    new_k_rope_sample = jnp.stack(kpe_s, 0)
    new_pool_sample = jnp.stack(pool_s, 0)
    new_sgu_v_sample = jnp.stack(v_s, 0)
    return (y_prompt, y_sample, new_kv_latent_prompt, new_k_rope_prompt, new_pool_prompt,
            new_kv_latent_sample, new_k_rope_sample, new_pool_sample, new_sgu_v_sample)
```

```python
import functools

import jax
import jax.numpy as jnp
from jax import lax
from jax.experimental import pallas as pl
from jax.experimental.pallas import tpu as pltpu

F32 = jnp.float32
BF16 = jnp.bfloat16

D_MODEL = 1024
D_POOL = 256
POOL_WINDOWS = (2, 4, 8, 16)
POOL_GROUP = 64
POOL_HIST = 15
D_SGU = 256
N_SGU_HEADS = 4
CHUNK = 128
N_HEADS = 8
QK_NOPE = 64
QK_ROPE = 32
QK_HEAD = QK_NOPE + QK_ROPE
V_HEAD = 64
Q_LORA = 384
KV_LORA = 256
ROPE_THETA = 10000.0
SM_SCALE = QK_HEAD ** -0.5
PAGE_SIZE = 128
N_EXPERTS = 8
EPS = 1e-6
NEG_INF = -1e30

LANES = 128
HEAD_PAD = 128
HALO = 16
VMEM_LIMIT = 52 * 1024 * 1024

C_P, C_U, C_V, C_QA, C_KVA, C_KPE, C_KPR, C_END = 0, 256, 512, 768, 1152, 1408, 1536, 1664


def _cparams(sem):
    return pltpu.CompilerParams(dimension_semantics=sem, vmem_limit_bytes=VMEM_LIMIT)


def _rms(x, g):
    return x * lax.rsqrt(jnp.mean(x * x, axis=-1, keepdims=True) + EPS) * g


def _silu(x):
    return x * jax.nn.sigmoid(x)


def _mod_spec(per_row, k, tm, tiles_per_seq):
    if per_row:
        return pl.BlockSpec((1, tm, D_MODEL), lambda i, *_: (0, i, k))
    return pl.BlockSpec((1, 1, D_MODEL), lambda i, *_: (i // tiles_per_seq, 0, k))


def _full(shape):
    nd = len(shape)
    return pl.BlockSpec(shape, lambda *_: (0,) * nd)


def _ada_kernel(c_ref, w_ref, b_ref, o_ref):
    s = _silu(c_ref[...]).astype(BF16)
    o_ref[0] = jnp.dot(s, w_ref[0].astype(BF16), preferred_element_type=F32) + b_ref[0]


def _ada_mod(c_all, w_ada, b_ada):
    depth, d, n = w_ada.shape
    r = c_all.shape[0]
    tn = 1024
    return pl.pallas_call(
        _ada_kernel,
        out_shape=jax.ShapeDtypeStruct((depth, r, n), F32),
        grid=(depth, n // tn),
        in_specs=[pl.BlockSpec((r, d), lambda l, j: (0, 0)),
                  pl.BlockSpec((1, d, tn), lambda l, j: (l, 0, j)),
                  pl.BlockSpec((1, 1, tn), lambda l, j: (l, 0, j))],
        out_specs=pl.BlockSpec((1, r, tn), lambda l, j: (l, 0, j)),
        compiler_params=_cparams(("arbitrary", "arbitrary")),
        name="ada_mod",
    )(c_all, w_ada, b_ada.reshape(depth, 1, n))


def _proj_kernel(absorbed, x_ref, sh_ref, sc_ref, gmix_ref, win_ref, gsgu_ref, gq_ref, gkv_ref,
                 wq_ref, wqr_ref, cq_ref, sq_ref, ck_ref, sk_ref, wa_ref, wb_ref,
                 p_ref, u_ref, vn_ref, ckv_ref, kpe_ref, oa_ref, ob_ref):
    x = x_ref[...]
    h = (_rms(x, gmix_ref[...]) * (1.0 + sc_ref[0]) + sh_ref[0]).astype(BF16)
    pr = jnp.dot(h, win_ref[...], preferred_element_type=F32)
    p_ref[...] = pr[:, C_P:C_U]
    u_ref[...] = pr[:, C_U:C_V]
    vn_ref[...] = _rms(pr[:, C_V:C_QA], gsgu_ref[...])
    ckv = _rms(pr[:, C_KVA:C_KPE], gkv_ref[...])
    ckv_ref[...] = ckv
    kpe = pr[:, C_KPE:C_KPR] * ck_ref[...] + pr[:, C_KPR:C_END] * sk_ref[...]
    kpe_ref[...] = kpe[:, :QK_ROPE]
    qn = _rms(pr[:, C_QA:C_KVA], gq_ref[...]).astype(BF16)
    qm = jnp.dot(qn, wq_ref[...], preferred_element_type=F32)
    qr = jnp.dot(qn, wqr_ref[...], preferred_element_type=F32)
    cq = cq_ref[...]
    sq = sq_ref[...]
    if absorbed:
        qs = []
        for hh in range(N_HEADS):
            sl = slice(hh * HEAD_PAD, (hh + 1) * HEAD_PAD)
            qh = (qm[:, sl] * cq + qr[:, sl] * sq).astype(BF16)
            qs.append(qh)
            oa_ref[:, hh * KV_LORA:(hh + 1) * KV_LORA] = jnp.dot(
                qh, wa_ref[hh], preferred_element_type=F32).astype(BF16)
        q = jnp.concatenate(qs, axis=1)
        ob_ref[...] = jnp.dot(q, wb_ref[...], preferred_element_type=F32).astype(BF16)
    else:
        for hh in range(N_HEADS):
            sl = slice(hh * HEAD_PAD, (hh + 1) * HEAD_PAD)
            oa_ref[:, sl] = (qm[:, sl] * cq + qr[:, sl] * sq).astype(BF16)
        kin = jnp.concatenate([ckv.astype(BF16), kpe.astype(BF16)], axis=1)
        ob_ref[:, :N_HEADS * HEAD_PAD] = jnp.dot(
            kin, wa_ref[...], preferred_element_type=F32).astype(BF16)
        ob_ref[:, N_HEADS * HEAD_PAD:] = jnp.dot(
            kin[:, :KV_LORA], wb_ref[...], preferred_element_type=F32).astype(BF16)


def _proj(x, mod, per_row, tiles_per_seq, tm, g_mix, w_in_p, g_sgu, g_q, g_kv, wq, wqr,
          cq, sq, ck, sk, wa, wb, absorbed):
    t = x.shape[0]
    n_tiles = t // tm
    tab_spec = pl.BlockSpec((tm, LANES), lambda i: (i % tiles_per_seq, 0))
    row = lambda w: pl.BlockSpec((tm, w), lambda i: (i, 0))
    if absorbed:
        oa_w, ob_w = N_HEADS * KV_LORA, N_HEADS * QK_ROPE
    else:
        oa_w, ob_w = N_HEADS * HEAD_PAD, N_HEADS * HEAD_PAD + N_HEADS * V_HEAD
    out_shape = (jax.ShapeDtypeStruct((t, D_POOL), F32), jax.ShapeDtypeStruct((t, D_SGU), F32),
                 jax.ShapeDtypeStruct((t, D_SGU), F32), jax.ShapeDtypeStruct((t, KV_LORA), F32),
                 jax.ShapeDtypeStruct((t, QK_ROPE), F32),
                 jax.ShapeDtypeStruct((t, oa_w), BF16), jax.ShapeDtypeStruct((t, ob_w), BF16))
    return pl.pallas_call(
        functools.partial(_proj_kernel, absorbed),
        out_shape=out_shape,
        grid=(n_tiles,),
        in_specs=[row(D_MODEL), _mod_spec(per_row, 0, tm, tiles_per_seq),
                  _mod_spec(per_row, 1, tm, tiles_per_seq),
                  _full(g_mix.shape), _full(w_in_p.shape), _full(g_sgu.shape), _full(g_q.shape),
                  _full(g_kv.shape), _full(wq.shape), _full(wqr.shape),
                  tab_spec, tab_spec, tab_spec, tab_spec, _full(wa.shape), _full(wb.shape)],
        out_specs=(row(D_POOL), row(D_SGU), row(D_SGU), row(KV_LORA), row(QK_ROPE),
                   row(oa_w), row(ob_w)),
        compiler_params=_cparams(("arbitrary",)),
        name="proj_absorbed" if absorbed else "proj",
    )(x, mod, mod, g_mix, w_in_p, g_sgu, g_q, g_kv, wq, wqr, cq, sq, ck, sk, wa, wb)


def _attn_kernel(q_ref, k_ref, v_ref, o_ref, m_ref, l_ref, acc_ref):
    i = pl.program_id(2)
    j = pl.program_id(3)
    tq = q_ref.shape[0]
    tk = k_ref.shape[0]

    @pl.when(j == 0)
    def _():
        m_ref[...] = jnp.full(m_ref.shape, -jnp.inf, F32)
        l_ref[...] = jnp.zeros(l_ref.shape, F32)
        acc_ref[...] = jnp.zeros(acc_ref.shape, F32)

    def step(masked):
        lane = lax.broadcasted_iota(jnp.int32, (tq, LANES), 1)
        first = lane < V_HEAD
        v = v_ref[...]
        alphas, pvs = [], []
        for hh in range(2):
            sl = slice(hh * HEAD_PAD, (hh + 1) * HEAD_PAD)
            s = lax.dot_general(q_ref[:, sl], k_ref[:, sl], (((1,), (1,)), ((), ())),
                                preferred_element_type=F32)
            if masked:
                r = lax.broadcasted_iota(jnp.int32, (tq, tk), 0)
                c = lax.broadcasted_iota(jnp.int32, (tq, tk), 1)
                s = jnp.where(c <= r, s, NEG_INF)
            m_prev = m_ref[hh]
            m_new = jnp.maximum(m_prev, jnp.max(s, axis=-1, keepdims=True))
            alpha = jnp.exp(m_prev - m_new)
            p = jnp.exp(s - m_new[:, :1])
            l_ref[hh] = alpha * l_ref[hh] + jnp.sum(p, axis=-1, keepdims=True)
            m_ref[hh] = m_new
            alphas.append(alpha)
            pvs.append(jnp.dot(p.astype(BF16), v, preferred_element_type=F32))
        acc_ref[...] = (jnp.where(first, alphas[0], alphas[1]) * acc_ref[...]
                        + jnp.where(first, pvs[0], pvs[1]))

    @pl.when(j < i)
    def _():
        step(False)

    @pl.when(j == i)
    def _():
        step(True)
        lane = lax.broadcasted_iota(jnp.int32, (tq, LANES), 1)
        l = jnp.where(lane < V_HEAD, l_ref[0], l_ref[1])
        o_ref[...] = (acc_ref[...] / l).astype(o_ref.dtype)


def _attn_prompt(q, kv, batch, seq, tq):
    nq = seq // tq
    pairs = N_HEADS // 2
    kv_blk = lambda b, g, i, j: (b * nq + jnp.minimum(i, j), g)
    v_blk = lambda b, g, i, j: (b * nq + jnp.minimum(i, j), N_HEADS + g)
    return pl.pallas_call(
        _attn_kernel,
        out_shape=jax.ShapeDtypeStruct((batch * seq, N_HEADS * V_HEAD), BF16),
        grid=(batch, pairs, nq, nq),
        in_specs=[pl.BlockSpec((tq, 2 * HEAD_PAD), lambda b, g, i, j: (b * nq + i, g)),
                  pl.BlockSpec((tq, 2 * HEAD_PAD), kv_blk),
                  pl.BlockSpec((tq, LANES), v_blk)],
        out_specs=pl.BlockSpec((tq, LANES), lambda b, g, i, j: (b * nq + i, g)),
        scratch_shapes=[pltpu.VMEM((2, tq, LANES), F32), pltpu.VMEM((2, tq, LANES), F32),
                        pltpu.VMEM((tq, LANES), F32)],
        compiler_params=_cparams(("arbitrary", "arbitrary", "arbitrary", "arbitrary")),
        name="attn_prompt",
    )(q, kv, kv)


def _attn_sample_kernel(layer, n_pages, chunk, pt_ref, qa_ref, qp_ref, cn_ref, pn_ref,
                        cc_hbm, cp_hbm, o_ref, cbuf, pbuf, sem):
    b = pl.program_id(0)
    nb = pl.num_programs(0)
    past = n_pages * PAGE_SIZE
    slot = b % 2

    def copies(bb, sl, pg):
        page = pt_ref[bb, pg]
        dst = pl.ds(pg * PAGE_SIZE, PAGE_SIZE)
        return (pltpu.make_async_copy(cc_hbm.at[layer, page], cbuf.at[sl, dst], sem.at[sl, 0]),
                pltpu.make_async_copy(cp_hbm.at[layer, page], pbuf.at[sl, dst], sem.at[sl, 1]))

    def start_all(bb, sl):
        def body(pg, carry):
            for cp in copies(bb, sl, pg):
                cp.start()
            return carry
        lax.fori_loop(0, n_pages, body, 0)

    def wait_all(bb, sl):
        def body(pg, carry):
            for cp in copies(bb, sl, pg):
                cp.wait()
            return carry
        lax.fori_loop(0, n_pages, body, 0)

    @pl.when(b == 0)
    def _():
        for sl in range(2):
            cbuf[sl, past:, :] = jnp.zeros((PAGE_SIZE, KV_LORA), F32)
            pbuf[sl, past:, :] = jnp.zeros((PAGE_SIZE, QK_ROPE), F32)
        start_all(0, 0)

    @pl.when(b + 1 < nb)
    def _():
        start_all(b + 1, 1 - slot)

    rows = cn_ref.shape[1]
    cbuf[slot, past:past + rows, :] = cn_ref[0]
    pbuf[slot, past:past + rows, :] = pn_ref[0]
    wait_all(b, slot)

    qa = qa_ref[0]
    qp = qp_ref[0]
    nr = qa.shape[0]
    contract = (((1,), (1,)), ((), ()))

    def online(carry, c, pe, mask):
        m, l, acc = carry
        cb = c.astype(BF16)
        s = (lax.dot_general(qa, cb, contract, preferred_element_type=F32)
             + lax.dot_general(qp, pe.astype(BF16), contract, preferred_element_type=F32))
        if mask is not None:
            s = jnp.where(mask, s, NEG_INF)
        m_new = jnp.maximum(m, jnp.max(s, axis=-1, keepdims=True))
        alpha = jnp.exp(m - m_new)
        p = jnp.exp(s - m_new)
        l = alpha * l + jnp.sum(p, axis=-1, keepdims=True)
        acc = alpha * acc + jnp.dot(p.astype(BF16), cb, preferred_element_type=F32)
        return m_new, l, acc

    carry = (jnp.full((nr, 1), -jnp.inf, F32), jnp.zeros((nr, 1), F32),
             jnp.zeros((nr, KV_LORA), F32))
    for ci in range(past // chunk):
        ks = slice(ci * chunk, (ci + 1) * chunk)
        carry = online(carry, cbuf[slot, ks, :], pbuf[slot, ks, :], None)
    qi = lax.broadcasted_iota(jnp.int32, (nr, PAGE_SIZE), 0) // N_HEADS
    kj = lax.broadcasted_iota(jnp.int32, (nr, PAGE_SIZE), 1)
    mask = (kj <= qi) & (kj < nr // N_HEADS)
    m, l, acc = online(carry, cbuf[slot, past:, :], pbuf[slot, past:, :], mask)
    o_ref[0] = (acc / l).astype(o_ref.dtype)


def _attn_sample(layer, page_table, qabs, qpe, ckv_new, kpe_new, cache_c, cache_pe):
    nb, n_pages = page_table.shape
    nr = qabs.shape[1]
    rows = ckv_new.shape[1]
    past = n_pages * PAGE_SIZE
    chunk = min(1024, past)
    grid_spec = pltpu.PrefetchScalarGridSpec(
        num_scalar_prefetch=1,
        grid=(nb,),
        in_specs=[pl.BlockSpec((1, nr, KV_LORA), lambda b, pt: (b, 0, 0)),
                  pl.BlockSpec((1, nr, QK_ROPE), lambda b, pt: (b, 0, 0)),
                  pl.BlockSpec((1, rows, KV_LORA), lambda b, pt: (b, 0, 0)),
                  pl.BlockSpec((1, rows, QK_ROPE), lambda b, pt: (b, 0, 0)),
                  pl.BlockSpec(memory_space=pl.ANY),
                  pl.BlockSpec(memory_space=pl.ANY)],
        out_specs=pl.BlockSpec((1, nr, KV_LORA), lambda b, pt: (b, 0, 0)),
        scratch_shapes=[pltpu.VMEM((2, past + PAGE_SIZE, KV_LORA), F32),
                        pltpu.VMEM((2, past + PAGE_SIZE, QK_ROPE), F32),
                        pltpu.SemaphoreType.DMA((2, 2))],
    )
    return pl.pallas_call(
        functools.partial(_attn_sample_kernel, layer, n_pages, chunk),
        out_shape=jax.ShapeDtypeStruct((nb, nr, KV_LORA), BF16),
        grid_spec=grid_spec,
        compiler_params=_cparams(("arbitrary",)),
        name="attn_sample",
    )(page_table, qabs, qpe, ckv_new, kpe_new, cache_c, cache_pe)


def _pool_kernel(tiles_per_seq, pos0, p_ref, prev_ref, w_ref, s_ref, o_ref):
    i = pl.program_id(0)
    tm = p_ref.shape[0]
    ti = i % tiles_per_seq
    p = p_ref[...]
    prev = jnp.where(ti == 0, 0.0, prev_ref[...])
    ext = jnp.concatenate([prev, p], axis=0)
    s2 = ext + pltpu.roll(ext, 1, 0)
    s4 = s2 + pltpu.roll(s2, 2, 0)
    s8 = s4 + pltpu.roll(s4, 4, 0)
    s16 = s8 + pltpu.roll(s8, 8, 0)
    pos1 = (pos0 + 1 + ti * tm + lax.broadcasted_iota(jnp.int32, (tm, 1), 0)).astype(F32)
    grp = lax.broadcasted_iota(jnp.int32, (tm, D_POOL), 1) // POOL_GROUP
    sums = (s2, s4, s8, s16)
    mean = None
    for g in reversed(range(len(POOL_WINDOWS))):
        mg = sums[g][HALO:] / jnp.minimum(float(POOL_WINDOWS[g]), pos1)
        mean = mg if mean is None else jnp.where(grp == g, mg, mean)
    d = (mean - p).astype(BF16)
    o_ref[...] = (jnp.dot(d, w_ref[...], preferred_element_type=F32) * s_ref[...]).astype(o_ref.dtype)


def _pool(p, w_bd, s_pool, tm, tiles_per_seq, pos0):
    t = p.shape[0]
    hb = tm // HALO
    return pl.pallas_call(
        functools.partial(_pool_kernel, tiles_per_seq, pos0),
        out_shape=jax.ShapeDtypeStruct((t, D_POOL), BF16),
        grid=(t // tm,),
        in_specs=[pl.BlockSpec((tm, D_POOL), lambda i: (i, 0)),
                  pl.BlockSpec((HALO, D_POOL), lambda i: (jnp.maximum(i * hb - 1, 0), 0)),
                  _full(w_bd.shape), _full(s_pool.shape)],
        out_specs=pl.BlockSpec((tm, D_POOL), lambda i: (i, 0)),
        compiler_params=_cparams(("arbitrary",)),
        name="pool_mix",
    )(p, p, w_bd, s_pool)


def _sgu_kernel(u_ref, vn_ref, w_ref, b_ref, o_ref):
    tm = u_ref.shape[0]
    grp = lax.broadcasted_iota(jnp.int32, (CHUNK, D_SGU), 1) // (D_SGU // N_SGU_HEADS)
    bias = b_ref[...]
    for c in range(tm // CHUNK):
        rs = slice(c * CHUNK, (c + 1) * CHUNK)
        vc = vn_ref[rs, :].astype(BF16)
        z = None
        for hh in reversed(range(N_SGU_HEADS)):
            zh = jnp.dot(w_ref[hh], vc, preferred_element_type=F32)
            z = zh if z is None else jnp.where(grp == hh, zh, z)
        o_ref[rs, :] = (u_ref[rs, :] * (z + bias)).astype(o_ref.dtype)


def _sgu(u, vn, wm, bias, tm):
    t = u.shape[0]
    return pl.pallas_call(
        _sgu_kernel,
        out_shape=jax.ShapeDtypeStruct((t, D_SGU), BF16),
        grid=(t // tm,),
        in_specs=[pl.BlockSpec((tm, D_SGU), lambda i: (i, 0)),
                  pl.BlockSpec((tm, D_SGU), lambda i: (i, 0)),
                  _full(wm.shape), _full(bias.shape)],
        out_specs=pl.BlockSpec((tm, D_SGU), lambda i: (i, 0)),
        compiler_params=_cparams(("arbitrary",)),
        name="sgu",
    )(u, vn, wm, bias)


def _matmul_kernel(x_ref, w_ref, o_ref):
    o_ref[...] = jnp.dot(x_ref[...], w_ref[...], preferred_element_type=F32).astype(o_ref.dtype)


def _matmul(x, w, out_dtype):
    m, _ = x.shape
    n = w.shape[1]
    return pl.pallas_call(
        _matmul_kernel,
        out_shape=jax.ShapeDtypeStruct((m, n), out_dtype),
        grid=(1,),
        in_specs=[_full(x.shape), _full(w.shape)],
        out_specs=_full((m, n)),
        compiler_params=_cparams(("arbitrary",)),
        name="matmul",
    )(x, w)


def _outproj_kernel(route, ya_ref, yb_ref, yc_ref, x_ref, gt_ref, sh_ref, sc_ref, g_ref, w_ref,
                    wr_hi_ref, wr_lo_ref, xo_ref, h_ref, lg_ref):
    y = jnp.concatenate([ya_ref[...], yb_ref[...], yc_ref[...]], axis=1)
    xn = x_ref[...] + gt_ref[0] * jnp.dot(y, w_ref[...], preferred_element_type=F32)
    xo_ref[...] = xn
    h = _rms(xn, g_ref[...]) * (1.0 + sc_ref[0]) + sh_ref[0]
    hi = h.astype(BF16)
    h_ref[...] = hi
    if route:
        lo = (h - hi.astype(F32)).astype(BF16)
        lg_ref[...] = (jnp.dot(hi, wr_hi_ref[...], preferred_element_type=F32)
                       + jnp.dot(lo, wr_hi_ref[...], preferred_element_type=F32)
                       + jnp.dot(hi, wr_lo_ref[...], preferred_element_type=F32))
    else:
        lg_ref[...] = jnp.zeros(lg_ref.shape, F32)


def _outproj(ya, yb, yc, x, mod, per_row, tiles_per_seq, tm, g_ffn, w_out, wr_hi, wr_lo, route):
    t = x.shape[0]
    row = lambda w: pl.BlockSpec((tm, w), lambda i: (i, 0))
    return pl.pallas_call(
        functools.partial(_outproj_kernel, route),
        out_shape=(jax.ShapeDtypeStruct((t, D_MODEL), F32), jax.ShapeDtypeStruct((t, D_MODEL), BF16),
                   jax.ShapeDtypeStruct((t, LANES), F32)),
        grid=(t // tm,),
        in_specs=[row(D_POOL), row(D_SGU), row(N_HEADS * V_HEAD), row(D_MODEL),
                  _mod_spec(per_row, 2, tm, tiles_per_seq), _mod_spec(per_row, 3, tm, tiles_per_seq),
                  _mod_spec(per_row, 4, tm, tiles_per_seq),
                  _full(g_ffn.shape), _full(w_out.shape), _full(wr_hi.shape), _full(wr_lo.shape)],
        out_specs=(row(D_MODEL), row(D_MODEL), row(LANES)),
        compiler_params=_cparams(("arbitrary",)),
        name="outproj",
    )(ya, yb, yc, x, mod, mod, mod, g_ffn, w_out, wr_hi, wr_lo)


def _ffn_kernel(final, h_ref, x_ref, gt_ref, wg_ref, wu_ref, wd_ref, gf_ref, o_ref, acc_ref):
    j = pl.program_id(1)

    @pl.when(j == 0)
    def _():
        acc_ref[...] = jnp.zeros(acc_ref.shape, F32)

    h = h_ref[...]
    g = jnp.dot(h, wg_ref[...], preferred_element_type=F32)
    u = jnp.dot(h, wu_ref[...], preferred_element_type=F32)
    a = (_silu(g) * u).astype(BF16)
    acc_ref[...] += jnp.dot(a, wd_ref[...], preferred_element_type=F32)

    @pl.when(j == pl.num_programs(1) - 1)
    def _():
        xo = x_ref[...] + gt_ref[0] * acc_ref[...]
        o_ref[...] = _rms(xo, gf_ref[...]) if final else xo


def _ffn(h, x, mod, per_row, tiles_per_seq, tm, wg, wu, wd, g_final, final):
    t = x.shape[0]
    dff = wg.shape[1]
    tf = dff // 2
    return pl.pallas_call(
        functools.partial(_ffn_kernel, final),
        out_shape=jax.ShapeDtypeStruct((t, D_MODEL), F32),
        grid=(t // tm, dff // tf),
        in_specs=[pl.BlockSpec((tm, D_MODEL), lambda i, j: (i, 0)),
                  pl.BlockSpec((tm, D_MODEL), lambda i, j: (i, 0)),
                  _mod_spec(per_row, 5, tm, tiles_per_seq),
                  pl.BlockSpec((D_MODEL, tf), lambda i, j: (0, j)),
                  pl.BlockSpec((D_MODEL, tf), lambda i, j: (0, j)),
                  pl.BlockSpec((tf, D_MODEL), lambda i, j: (j, 0)),
                  _full(g_final.shape)],
        out_specs=pl.BlockSpec((tm, D_MODEL), lambda i, j: (i, 0)),
        scratch_shapes=[pltpu.VMEM((tm, D_MODEL), F32)],
        compiler_params=_cparams(("arbitrary", "arbitrary")),
        name="ffn_dense",
    )(h, x, mod, wg, wu, wd, g_final)


def _route_kernel(lg_ref, comb_ref):
    lg = lg_ref[...]
    lane = lax.broadcasted_iota(jnp.int32, lg.shape, 1)
    lg = jnp.where(lane < N_EXPERTS, lg, -jnp.inf)
    m1 = jnp.max(lg, axis=-1, keepdims=True)
    i1 = jnp.min(jnp.where(lg == m1, lane, LANES), axis=-1, keepdims=True)
    rest = jnp.where(lane == i1, -jnp.inf, lg)
    m2 = jnp.max(rest, axis=-1, keepdims=True)
    i2 = jnp.min(jnp.where(rest == m2, lane, LANES), axis=-1, keepdims=True)
    e2 = jnp.exp(m2 - m1)
    g1 = 1.0 / (1.0 + e2)
    g2 = e2 / (1.0 + e2)
    comb_ref[...] = jnp.where(lane == i1, g1, 0.0) + jnp.where(lane == i2, g2, 0.0)


def _route(logits, tm):
    t = logits.shape[0]
    return pl.pallas_call(
        _route_kernel,
        out_shape=jax.ShapeDtypeStruct((t, LANES), F32),
        grid=(t // tm,),
        in_specs=[pl.BlockSpec((tm, LANES), lambda i: (i, 0))],
        out_specs=pl.BlockSpec((tm, LANES), lambda i: (i, 0)),
        compiler_params=_cparams(("arbitrary",)),
        name="route",
    )(logits)


def _moe_kernel(final, h_ref, x_ref, gt_ref, comb_ref, wg_ref, wu_ref, wd_ref, gf_ref, o_ref, acc_ref):
    e = pl.program_id(1)
    j = pl.program_id(2)

    @pl.when((e == 0) & (j == 0))
    def _():
        acc_ref[...] = jnp.zeros(acc_ref.shape, F32)

    comb = comb_ref[...]
    lane = lax.broadcasted_iota(jnp.int32, comb.shape, 1)
    ce = jnp.sum(jnp.where(lane == e, comb, 0.0), axis=-1, keepdims=True)
    h = h_ref[...]
    g = jnp.dot(h, wg_ref[0], preferred_element_type=F32)
    u = jnp.dot(h, wu_ref[0], preferred_element_type=F32)
    a = (_silu(g) * u * ce).astype(BF16)
    acc_ref[...] += jnp.dot(a, wd_ref[0], preferred_element_type=F32)

    @pl.when((e == pl.num_programs(1) - 1) & (j == pl.num_programs(2) - 1))
    def _():
        xo = x_ref[...] + gt_ref[0] * acc_ref[...]
        o_ref[...] = _rms(xo, gf_ref[...]) if final else xo


def _moe(h, x, mod, per_row, tiles_per_seq, tm, comb, wg, wu, wd, g_final, final):
    t = x.shape[0]
    ne, _, dff = wg.shape
    tf = dff // 4
    rowspec = lambda w: pl.BlockSpec((tm, w), lambda i, e, j: (i, 0))
    return pl.pallas_call(
        functools.partial(_moe_kernel, final),
        out_shape=jax.ShapeDtypeStruct((t, D_MODEL), F32),
        grid=(t // tm, ne, dff // tf),
        in_specs=[rowspec(D_MODEL), rowspec(D_MODEL), _mod_spec(per_row, 5, tm, tiles_per_seq),
                  rowspec(LANES),
                  pl.BlockSpec((1, D_MODEL, tf), lambda i, e, j: (e, 0, j)),
                  pl.BlockSpec((1, D_MODEL, tf), lambda i, e, j: (e, 0, j)),
                  pl.BlockSpec((1, tf, D_MODEL), lambda i, e, j: (e, j, 0)),
                  _full(g_final.shape)],
        out_specs=rowspec(D_MODEL),
        scratch_shapes=[pltpu.VMEM((tm, D_MODEL), F32)],
        compiler_params=_cparams(("arbitrary", "arbitrary", "arbitrary")),
        name="moe_dense",
    )(h, x, mod, comb, wg, wu, wd, g_final)


def _rope_tables(pos):
    half = QK_ROPE // 2
    freq = ROPE_THETA ** (-jnp.arange(half, dtype=F32) / half)
    ang = pos.astype(F32)[:, None] * freq[None, :]
    cos, sin = jnp.cos(ang), jnp.sin(ang)
    n = pos.shape[0]
    z = lambda w: jnp.zeros((n, w), F32)
    cq = SM_SCALE * jnp.concatenate([jnp.ones((n, QK_NOPE), F32), cos, cos, z(LANES - QK_HEAD)], axis=1)
    sq = SM_SCALE * jnp.concatenate([z(QK_NOPE), sin, sin, z(LANES - QK_HEAD)], axis=1)
    ck = jnp.concatenate([cos, cos, z(LANES - QK_ROPE)], axis=1)
    sk = jnp.concatenate([sin, sin, z(LANES - QK_ROPE)], axis=1)
    return cq, sq, ck, sk


def _rot_cols(w):
    half = QK_ROPE // 2
    return jnp.concatenate([-w[..., half:], w[..., :half]], axis=-1)


def _layer_weights(l, w_in, w_pool, s_pool, g_sgu, w_sgu, b_sgu, g_q, w_qb, g_kv, w_uk, w_uv, w_out,
                   dec_seq):
    half = QK_ROPE // 2
    wi = w_in[l]
    kpe_cols = wi[:, C_KPE:C_KPE + QK_ROPE]
    zpad = jnp.zeros((D_MODEL, LANES - QK_ROPE), F32)
    w_in_p = jnp.concatenate([wi, zpad, _rot_cols(kpe_cols), zpad], axis=1).astype(BF16)

    wq = w_qb[l].reshape(Q_LORA, N_HEADS, QK_HEAD)
    zq = jnp.zeros((Q_LORA, N_HEADS, LANES - QK_HEAD), F32)
    wq_main = jnp.concatenate([wq, zq], axis=-1).reshape(Q_LORA, N_HEADS * HEAD_PAD).astype(BF16)
    wq_rot = jnp.concatenate([jnp.zeros((Q_LORA, N_HEADS, QK_NOPE), F32), _rot_cols(wq[..., QK_NOPE:]), zq],
                             axis=-1).reshape(Q_LORA, N_HEADS * HEAD_PAD).astype(BF16)

    uk = jnp.transpose(w_uk[l], (1, 0, 2))
    uk_p = jnp.concatenate([uk, jnp.zeros((KV_LORA, N_HEADS, HEAD_PAD - QK_NOPE), F32)], axis=-1)
    place = jnp.concatenate([jnp.zeros((QK_ROPE, QK_NOPE), F32), jnp.eye(QK_ROPE, dtype=F32),
                             jnp.zeros((QK_ROPE, HEAD_PAD - QK_HEAD), F32)], axis=1)
    place = jnp.concatenate([place, jnp.zeros((LANES - QK_ROPE, HEAD_PAD), F32)], axis=0)
    place = jnp.tile(place[:, None, :], (1, N_HEADS, 1))
    wk = jnp.concatenate([uk_p, place], axis=0).reshape(KV_LORA + LANES, N_HEADS * HEAD_PAD).astype(BF16)
    wv = jnp.transpose(w_uv[l], (1, 0, 2)).reshape(KV_LORA, N_HEADS * V_HEAD).astype(BF16)

    wabs = jnp.concatenate([jnp.transpose(w_uk[l], (0, 2, 1)),
                            jnp.zeros((N_HEADS, HEAD_PAD - QK_NOPE, KV_LORA), F32)], axis=1).astype(BF16)
    sel = jnp.concatenate([jnp.zeros((QK_NOPE, QK_ROPE), F32), jnp.eye(QK_ROPE, dtype=F32),
                           jnp.zeros((HEAD_PAD - QK_HEAD, QK_ROPE), F32)], axis=0)
    wsel = jnp.einsum('hg,dr->hdgr', jnp.eye(N_HEADS, dtype=F32), sel).reshape(
        N_HEADS * HEAD_PAD, N_HEADS * QK_ROPE).astype(BF16)
    wuv_bd = jnp.einsum('hg,hcd->hcgd', jnp.eye(N_HEADS, dtype=F32), w_uv[l]).reshape(
        N_HEADS * KV_LORA, N_HEADS * V_HEAD).astype(BF16)

    ng = len(POOL_WINDOWS)
    w_bd = jnp.einsum('gk,gcd->gckd', jnp.eye(ng, dtype=F32), w_pool[l]).reshape(D_POOL, D_POOL).astype(BF16)

    tril = jnp.tril(jnp.ones((CHUNK, CHUNK), F32))
    wm_p = (w_sgu[l] * tril).astype(BF16)
    head_w = D_SGU // N_SGU_HEADS
    bias_p = jnp.repeat(b_sgu[l].T, head_w, axis=1)
    reps = CHUNK // dec_seq
    wm_s = jnp.einsum('ab,hij->haibj', jnp.eye(reps, dtype=F32),
                      (w_sgu[l] * tril)[:, :dec_seq, :dec_seq]).reshape(N_SGU_HEADS, CHUNK, CHUNK).astype(BF16)
    bias_s = jnp.tile(jnp.repeat(b_sgu[l][:, :dec_seq].T, head_w, axis=1), (reps, 1))
    return dict(w_in_p=w_in_p, wq=wq_main, wqr=wq_rot, wk=wk, wv=wv, wabs=wabs, wsel=wsel, wuv_bd=wuv_bd,
                w_bd=w_bd, s_pool=s_pool[l][None], wm_p=wm_p, bias_p=bias_p, wm_s=wm_s, bias_s=bias_s,
                g_sgu=g_sgu[l][None], g_q=g_q[l][None], g_kv=g_kv[l][None], w_out=w_out[l].astype(BF16))


def kernel(x_prompt, x_sample, cache_kv_latent, cache_k_rope, state_pool, page_table, c_prompt, c_sample,
           g_mix, g_ffn, w_ada, b_ada, w_in, w_pool, s_pool, g_sgu, w_sgu, b_sgu, g_q, w_qb, g_kv,
           w_uk, w_uv, w_out, w_ff_gate, w_ff_up, w_ff_down, w_router, w_e_gate, w_e_up, w_e_down, g_final):
    batch, seq, _ = x_prompt.shape
    db, ns, _ = x_sample.shape
    depth = w_in.shape[0]
    n_pages = page_table.shape[1]
    tp = batch * seq
    ts = db * ns
    assert ts % CHUNK == 0 and CHUNK % ns == 0 and seq % CHUNK == 0
    tm_p = min(512, seq)
    tps_p = seq // tm_p
    tm_s = min(512, ts)
    tq = min(512, seq)

    cpad = 8
    c_all = jnp.concatenate([c_prompt, jnp.zeros((cpad - batch, D_MODEL), F32), c_sample], axis=0)
    mods = _ada_mod(c_all, w_ada, b_ada)

    tabs_p = _rope_tables(jnp.arange(seq))
    tabs_s = tuple(jnp.tile(tb, (db, 1)) for tb in _rope_tables(n_pages * PAGE_SIZE + jnp.arange(ns)))

    xp = x_prompt.reshape(tp, D_MODEL)
    xs = x_sample.reshape(ts, D_MODEL)
    gfin = g_final[None]
    outs = {k: [] for k in ("ckv_p", "kpe_p", "pool_p", "ckv_s", "kpe_s", "pool_s", "v_s")}

    for l in range(depth):
        lw = _layer_weights(l, w_in, w_pool, s_pool, g_sgu, w_sgu, b_sgu, g_q, w_qb, g_kv, w_uk, w_uv,
                            w_out, ns)
        final = l == depth - 1
        moe_layer = l % 2 == 1
        gm, gf = g_mix[l][None], g_ffn[l][None]
        mod_p = mods[l, :batch].reshape(batch, 1, 6 * D_MODEL)
        mod_s = jnp.repeat(mods[l, cpad:cpad + db], ns, axis=0)[None]
        if moe_layer:
            i = l // 2
            wr = jnp.concatenate([w_router[i], jnp.zeros((D_MODEL, LANES - N_EXPERTS), F32)], axis=1)
            wr_hi = wr.astype(BF16)
            wr_lo = (wr - wr_hi.astype(F32)).astype(BF16)
            ffw = (w_e_gate[i].astype(BF16), w_e_up[i].astype(BF16), w_e_down[i].astype(BF16))
        else:
            i = l // 2
            wr_hi = wr_lo = jnp.zeros((D_MODEL, LANES), BF16)
            ffw = (w_ff_gate[i].astype(BF16), w_ff_up[i].astype(BF16), w_ff_down[i].astype(BF16))

        p, u, vn, ckv, kpe, q, kv = _proj(xp, mod_p, False, tps_p, tm_p, gm, lw["w_in_p"], lw["g_sgu"],
                                          lw["g_q"], lw["g_kv"], lw["wq"], lw["wqr"], *tabs_p,
                                          lw["wk"], lw["wv"], absorbed=False)
        ya = _pool(p, lw["w_bd"], lw["s_pool"], tm_p, tps_p, 0)
        yb = _sgu(u, vn, lw["wm_p"], lw["bias_p"], tm_p)
        yc = _attn_prompt(q, kv, batch, seq, tq)
        xp, hp, lgp = _outproj(ya, yb, yc, xp, mod_p, False, tps_p, tm_p, gf, lw["w_out"], wr_hi, wr_lo,
                               moe_layer)
        if moe_layer:
            tm_e = min(1024, seq)
            xp = _moe(hp, xp, mod_p, False, seq // tm_e, tm_e, _route(lgp, tm_p), *ffw, gfin, final)
        else:
            xp = _ffn(hp, xp, mod_p, False, tps_p, tm_p, *ffw, gfin, final)
        outs["ckv_p"].append(ckv.reshape(batch, seq, KV_LORA))
        outs["kpe_p"].append(kpe.reshape(batch, seq, QK_ROPE))
        outs["pool_p"].append(p.reshape(batch, seq, D_POOL)[:, seq - POOL_HIST:])

        p, u, vn, ckv, kpe, qabs, qpe = _proj(xs, mod_s, True, ts // tm_s, tm_s, gm, lw["w_in_p"],
                                              lw["g_sgu"], lw["g_q"], lw["g_kv"], lw["wq"], lw["wqr"],
                                              *tabs_s, lw["wabs"], lw["wsel"], absorbed=True)
        p_ext = jnp.concatenate([state_pool[l], p.reshape(db, ns, D_POOL)], axis=1)
        ext_rows = db * (POOL_HIST + ns)
        assert ext_rows % HALO == 0
        ya_ext = _pool(p_ext.reshape(ext_rows, D_POOL), lw["w_bd"], lw["s_pool"], ext_rows, 1, HALO)
        ya = ya_ext.reshape(db, POOL_HIST + ns, D_POOL)[:, POOL_HIST:].reshape(ts, D_POOL)
        yb = _sgu(u, vn, lw["wm_s"], lw["bias_s"], tm_s)
        rows_pad = 8
        cn = jnp.pad(ckv.reshape(db, ns, KV_LORA), ((0, 0), (0, rows_pad - ns), (0, 0)))
        pn = jnp.pad(kpe.reshape(db, ns, QK_ROPE), ((0, 0), (0, rows_pad - ns), (0, 0)))
        o_lat = _attn_sample(l, page_table, qabs.reshape(db, ns * N_HEADS, KV_LORA),
                             qpe.reshape(db, ns * N_HEADS, QK_ROPE), cn, pn, cache_kv_latent, cache_k_rope)
        yc = _matmul(o_lat.reshape(ts, N_HEADS * KV_LORA), lw["wuv_bd"], BF16)
        xs, hs, lgs = _outproj(ya, yb, yc, xs, mod_s, True, ts // tm_s, tm_s, gf, lw["w_out"], wr_hi, wr_lo,
                               moe_layer)
        if moe_layer:
            xs = _moe(hs, xs, mod_s, True, ts // tm_s, tm_s, _route(lgs, tm_s), *ffw, gfin, final)
        else:
            xs = _ffn(hs, xs, mod_s, True, ts // tm_s, tm_s, *ffw, gfin, final)
        outs["ckv_s"].append(ckv.reshape(db, ns, KV_LORA))
        outs["kpe_s"].append(kpe.reshape(db, ns, QK_ROPE))
        outs["pool_s"].append(p_ext[:, ns:])
        outs["v_s"].append(vn.reshape(db, ns, D_SGU))

    st = lambda k: jnp.stack(outs[k], 0)
    return (xp.reshape(batch, seq, D_MODEL), xs.reshape(db, ns, D_MODEL),
            st("ckv_p"), st("kpe_p"), st("pool_p"), st("ckv_s"), st("kpe_s"), st("pool_s"), st("v_s"))
```
